```python
import math
import jax, jax.numpy as jnp
from jax import lax
import numpy as np

D_MODEL = 1024
BATCH = 2
SEQ = 8192
DEPTH = 1

N_META = 16
BLOCK = 128
HEAD_DIM = 64
ROT_DIM = HEAD_DIM // 4
ROPE_THETA = 500000.0
DSA_HEADS = 8
KV_RANK = 256
IDX_HEADS = 8
IDX_DIM = 64
TOPK_MAX = 256
FOX_HEADS = 8
DSA_W = DSA_HEADS * HEAD_DIM
FOX_W = FOX_HEADS * HEAD_DIM
N_GROUPS = 4
EXPERTS_PER_GROUP = 8
N_EXPERTS = N_GROUPS * EXPERTS_PER_GROUP
TOPK_IN_GROUP = 2
D_EXPERT = 256
MOE_BLOCK = 128
DN_ALPHA = (2.0 * DEPTH) ** 0.25
DN_BETA = (8.0 * DEPTH) ** -0.25
LN_EPS = 1e-5
RMS_EPS = 1e-6
NEG = -1e30
IN_SPLITS = (DSA_W, KV_RANK, IDX_HEADS * IDX_DIM, IDX_DIM, IDX_HEADS, FOX_W, FOX_W, FOX_W, FOX_HEADS, D_MODEL, D_MODEL)
IN_WIDTH = sum(IN_SPLITS)

kernel_name = "hybrid_dsa_fox_gated_hmoe_deepnorm"


def layer_norm(x, g, b):
    xf = x.astype(jnp.float32)
    mu = xf.mean(-1, keepdims=True)
    var = jnp.square(xf - mu).mean(-1, keepdims=True)
    return ((xf - mu) * lax.rsqrt(var + LN_EPS)).astype(x.dtype) * g + b


def rms_norm(x, g):
    xf = x.astype(jnp.float32)
    return (xf * lax.rsqrt(jnp.square(xf).mean(-1, keepdims=True) + RMS_EPS)).astype(x.dtype) * g


def partial_rope(x, pos):
    half = ROT_DIM // 2
    inv = ROPE_THETA ** (-jnp.arange(half, dtype=jnp.float32) / half)
    ang = pos.astype(jnp.float32)[:, None] * inv[None, :]
    cos = jnp.cos(ang)[None, :, None, :]
    sin = jnp.sin(ang)[None, :, None, :]
    xr = x[..., :ROT_DIM].astype(jnp.float32)
    x1, x2 = xr[..., :half], xr[..., half:]
    rot = jnp.concatenate([x1 * cos - x2 * sin, x2 * cos + x1 * sin], axis=-1).astype(x.dtype)
    return jnp.concatenate([rot, x[..., ROT_DIM:]], axis=-1)


def to_blocks(a):
    B, Lp = a.shape[0], a.shape[1]
    return jnp.moveaxis(a.reshape((B, Lp // BLOCK, BLOCK) + a.shape[2:]), 1, 0)


def from_blocks(a):
    nb, B = a.shape[0], a.shape[1]
    return jnp.moveaxis(a, 0, 1).reshape((B, nb * BLOCK) + a.shape[3:])


def dsa_attention(q, k, v, qi, ki, wi, n_keys):
    Lp = q.shape[1]
    dh = q.shape[-1]
    pad = Lp - n_keys
    topk = min(TOPK_MAX, n_keys // 4)
    kpos = jnp.arange(Lp)
    ki32 = ki.astype(jnp.float32)

    def blk(args):
        qb, qib, wib, b0 = args
        qpos = b0 * BLOCK + jnp.arange(BLOCK)
        valid = (kpos[None, :] <= qpos[:, None]) & (kpos[None, :] >= pad)
        s_idx = jnp.einsum('bqhd,bkd->bqhk', qib.astype(jnp.float32), ki32) * (IDX_DIM ** -0.5)
        s_idx = jnp.einsum('bqhk,bqh->bqk', jax.nn.relu(s_idx), wib.astype(jnp.float32))
        s_idx = jnp.where(valid[None], s_idx, NEG)
        _, sel = lax.top_k(s_idx, topk)
        sel_valid = (sel <= qpos[None, :, None]) & (sel >= pad)
        ks = jax.vmap(lambda kk, ii: kk[ii])(k, sel)
        vs = jax.vmap(lambda vv, ii: vv[ii])(v, sel)
        s = jnp.einsum('bqhd,bqkhd->bhqk', qb.astype(jnp.float32), ks.astype(jnp.float32)) * (dh ** -0.5)
        s = jnp.where(sel_valid[:, None], s, NEG)
        p = jax.nn.softmax(s, axis=-1)
        o = jnp.einsum('bhqk,bqkhd->bqhd', p, vs.astype(jnp.float32))
        return o.astype(q.dtype)

    nb = Lp // BLOCK
    out = lax.map(blk, (to_blocks(q), to_blocks(qi), to_blocks(wi), jnp.arange(nb)))
    return from_blocks(out)


def fox_attention(q, k, v, log_f, n_keys):
    Lp = q.shape[1]
    dh = q.shape[-1]
    pad = Lp - n_keys
    kpos = jnp.arange(Lp)
    c = jnp.cumsum(log_f, axis=1)
    c_k = jnp.transpose(c, (0, 2, 1))
    k32 = k.astype(jnp.float32)
    v32 = v.astype(jnp.float32)

    def blk(args):
        qb, cqb, b0 = args
        qpos = b0 * BLOCK + jnp.arange(BLOCK)
        valid = (kpos[None, :] <= qpos[:, None]) & (kpos[None, :] >= pad)
        s = jnp.einsum('bqhd,bkhd->bhqk', qb.astype(jnp.float32), k32) * (dh ** -0.5)
        s = s + jnp.transpose(cqb, (0, 2, 1))[..., None] - c_k[:, :, None, :]
        s = jnp.where(valid[None, None], s, NEG)
        p = jax.nn.softmax(s, axis=-1)
        o = jnp.einsum('bhqk,bkhd->bqhd', p, v32)
        return o.astype(q.dtype)

    nb = Lp // BLOCK
    out = lax.map(blk, (to_blocks(q), to_blocks(c), jnp.arange(nb)))
    return from_blocks(out)


def token_mixer(h, pos, w_in, b_forget, kv_norm_g, w_kv_up, w_branch_dsa, w_branch_fox, w_out):
    B, T, _ = h.shape
    pad = (-N_META) % BLOCK
    proj = h @ w_in
    offs = np.cumsum(IN_SPLITS)[:-1].tolist()
    (q_a, c_kv, q_i, k_i, w_i, q_f, k_f, v_f, f_lg, g_a, g_f) = jnp.split(proj, offs, axis=-1)

    def heads(t, n):
        return t.reshape(B, T, n, -1)

    q_a = partial_rope(heads(q_a, DSA_HEADS), pos)
    kv = rms_norm(c_kv, kv_norm_g) @ w_kv_up
    k_a, v_a = jnp.split(kv, 2, axis=-1)
    k_a = partial_rope(heads(k_a, DSA_HEADS), pos)
    v_a = heads(v_a, DSA_HEADS)
    q_i = partial_rope(heads(q_i, IDX_HEADS), pos)
    k_i = partial_rope(k_i[:, :, None, :], pos)[:, :, 0]
    w_i = w_i * (IDX_HEADS ** -0.5)
    q_f, k_f, v_f = heads(q_f, FOX_HEADS), heads(k_f, FOX_HEADS), heads(v_f, FOX_HEADS)
    log_f = jax.nn.log_sigmoid((f_lg + b_forget).astype(jnp.float32))

    def pf(t):
        return jnp.pad(t, [(0, 0), (pad, 0)] + [(0, 0)] * (t.ndim - 2))

    y_a = dsa_attention(pf(q_a), pf(k_a), pf(v_a), pf(q_i), pf(k_i), pf(w_i), T)[:, pad:].reshape(B, T, DSA_W)
    y_f = fox_attention(pf(q_f), pf(k_f), pf(v_f), pf(log_f), T)[:, pad:].reshape(B, T, FOX_W)
    merged = jax.nn.sigmoid(g_a) * (y_a @ w_branch_dsa) + jax.nn.sigmoid(g_f) * (y_f @ w_branch_fox)
    return merged @ w_out


def hierarchical_moe(h, w_rg, b_rg, w_re, b_re, w_gate, w_up, w_down):
    B, T, D = h.shape
    xf = h.reshape(-1, D)
    N = xf.shape[0]
    g_logits = (xf @ w_rg).astype(jnp.float32) + b_rg
    g_prob = jax.nn.softmax(g_logits, axis=-1)
    g_sel = jnp.argmax(g_logits, axis=-1)
    rows = jnp.arange(N)
    p_group = g_prob[rows, g_sel][:, None]
    e_logits = ((xf @ w_re).astype(jnp.float32) + b_re).reshape(N, N_GROUPS, EXPERTS_PER_GROUP)
    e_logits = e_logits[rows, g_sel]
    top_l, top_i = lax.top_k(e_logits, TOPK_IN_GROUP)
    gate = p_group * jax.nn.softmax(top_l, axis=-1)
    eid = (g_sel[:, None] * EXPERTS_PER_GROUP + top_i).reshape(-1)
    tok = jnp.repeat(rows, TOPK_IN_GROUP)
    gw = gate.reshape(-1)
    A = N * TOPK_IN_GROUP
    order = jnp.argsort(eid)
    eid_s, tok_s, gw_s = eid[order], tok[order], gw[order]
    counts = jnp.zeros((N_EXPERTS,), jnp.int32).at[eid].add(1)
    start = jnp.cumsum(counts) - counts
    pcounts = (counts + MOE_BLOCK - 1) // MOE_BLOCK * MOE_BLOCK
    pend = jnp.cumsum(pcounts)
    pstart = pend - pcounts
    dest = pstart[eid_s] + jnp.arange(A) - start[eid_s]
    n_blocks = (A + N_EXPERTS * (MOE_BLOCK - 1) + MOE_BLOCK - 1) // MOE_BLOCK
    R = n_blocks * MOE_BLOCK
    buf_x = jnp.zeros((R, D), h.dtype).at[dest].set(xf[tok_s])
    buf_w = jnp.zeros((R,), jnp.float32).at[dest].set(gw_s.astype(jnp.float32))
    buf_tok = jnp.zeros((R,), jnp.int32).at[dest].set(tok_s)
    blk_e = jnp.minimum(jnp.searchsorted(pend, jnp.arange(n_blocks) * MOE_BLOCK, side='right'), N_EXPERTS - 1)

    def expert_block(args):
        xb, e = args
        return (jax.nn.silu(xb @ w_gate[e]) * (xb @ w_up[e])) @ w_down[e]

    y = lax.map(expert_block, (buf_x.reshape(n_blocks, MOE_BLOCK, D), blk_e)).reshape(R, D)
    out = jnp.zeros((N, D), jnp.float32).at[buf_tok].add(y.astype(jnp.float32) * buf_w[:, None])
    return out.astype(h.dtype).reshape(B, T, D)


def setup_inputs(seed: int = 0) -> dict:
    key = jax.random.key(seed)
    ks = jax.random.split(key, 24)
    n = jax.random.normal
    f32 = jnp.float32
    L = DEPTH
    return {
        "x": n(ks[0], (BATCH, SEQ, D_MODEL), f32),
        "meta_tokens": n(ks[1], (N_META, D_MODEL), f32),
        "emb_ln_g": 1.0 + 0.02 * n(ks[2], (D_MODEL,), f32),
        "emb_ln_b": 0.02 * n(ks[3], (D_MODEL,), f32),
        "w_in": n(ks[4], (L, D_MODEL, IN_WIDTH), f32) * D_MODEL ** -0.5,
        "b_forget": 3.0 + 0.5 * n(ks[5], (L, FOX_HEADS), f32),
        "kv_norm_g": 1.0 + 0.02 * n(ks[6], (L, KV_RANK), f32),
        "w_kv_up": n(ks[7], (L, KV_RANK, 2 * DSA_W), f32) * KV_RANK ** -0.5,
        "w_branch_dsa": n(ks[8], (L, DSA_W, D_MODEL), f32) * DSA_W ** -0.5,
        "w_branch_fox": n(ks[9], (L, FOX_W, D_MODEL), f32) * FOX_W ** -0.5,
        "w_out": n(ks[10], (L, D_MODEL, D_MODEL), f32) * (D_MODEL ** -0.5 * DN_BETA),
        "ln1_g": 1.0 + 0.02 * n(ks[11], (L, D_MODEL), f32),
        "ln1_b": 0.02 * n(ks[12], (L, D_MODEL), f32),
        "w_router_group": n(ks[13], (L, D_MODEL, N_GROUPS), f32) * D_MODEL ** -0.5,
        "b_router_group": 0.01 * n(ks[14], (L, N_GROUPS), f32),
        "w_router_expert": n(ks[15], (L, D_MODEL, N_EXPERTS), f32) * D_MODEL ** -0.5,
        "b_router_expert": 0.01 * n(ks[16], (L, N_EXPERTS), f32),
        "w_gate": n(ks[17], (L, N_EXPERTS, D_MODEL, D_EXPERT), f32) * D_MODEL ** -0.5,
        "w_up": n(ks[18], (L, N_EXPERTS, D_MODEL, D_EXPERT), f32) * D_MODEL ** -0.5,
        "w_down": n(ks[19], (L, N_EXPERTS, D_EXPERT, D_MODEL), f32) * (D_EXPERT ** -0.5 * DN_BETA),
        "ln2_g": 1.0 + 0.02 * n(ks[20], (L, D_MODEL), f32),
        "ln2_b": 0.02 * n(ks[21], (L, D_MODEL), f32),
    }


def reference(x, meta_tokens, emb_ln_g, emb_ln_b, w_in, b_forget, kv_norm_g, w_kv_up, w_branch_dsa,
              w_branch_fox, w_out, ln1_g, ln1_b, w_router_group, b_router_group, w_router_expert,
              b_router_expert, w_gate, w_up, w_down, ln2_g, ln2_b):
    B = x.shape[0]
    meta = jnp.broadcast_to(meta_tokens.astype(x.dtype)[None], (B, N_META, x.shape[-1]))
    h = layer_norm(jnp.concatenate([meta, x], axis=1), emb_ln_g, emb_ln_b)
    pos = jnp.arange(h.shape[1])
    for l in range(DEPTH):
        mix = token_mixer(h, pos, w_in[l], b_forget[l], kv_norm_g[l], w_kv_up[l],
                          w_branch_dsa[l], w_branch_fox[l], w_out[l])
        h = layer_norm(DN_ALPHA * h + mix, ln1_g[l], ln1_b[l])
        ffn = hierarchical_moe(h, w_router_group[l], b_router_group[l], w_router_expert[l],
                               b_router_expert[l], w_gate[l], w_up[l], w_down[l])
        h = layer_norm(DN_ALPHA * h + ffn, ln2_g[l], ln2_b[l])
    return h[:, N_META:]
```

```python
import functools

import numpy as np
import jax
import jax.numpy as jnp
from jax import lax
from jax.experimental import pallas as pl
from jax.experimental.pallas import tpu as pltpu

N_META = 16
HEAD_DIM = 64
ROT_DIM = HEAD_DIM // 4
ROPE_THETA = 500000.0
N_HEADS = 8
KV_RANK = 256
IDX_DIM = 64
TOPK_MAX = 256
HEADS_W = N_HEADS * HEAD_DIM
N_GROUPS = 4
EXPERTS_PER_GROUP = 8
N_EXPERTS = N_GROUPS * EXPERTS_PER_GROUP
D_EXPERT = 256
DEPTH = 1
DN_ALPHA = (2.0 * DEPTH) ** 0.25
LN_EPS = 1e-5
RMS_EPS = 1e-6
NEG = -1e30

LANES = 128
SUBLANES = 8
TM = 256
MOE_ROWS = 256
EXPERT_LANE0 = N_GROUPS
VMEM_LIMIT = 56 * 1024 * 1024

F32 = jnp.float32
BF16 = jnp.bfloat16
I32 = jnp.int32

_NEG_BITS = int(np.array(NEG, np.float32).view(np.int32))
NEG_KEY = _NEG_BITS ^ ((_NEG_BITS >> 31) & 0x7FFFFFFF)
INT_MIN = -(2 ** 31)


def _dot(a, b, precision=None):
    return jnp.dot(a, b, preferred_element_type=F32, precision=precision)


def _dot_nt(a, b):
    return lax.dot_general(a, b, (((1,), (1,)), ((), ())), preferred_element_type=F32)


def _layer_norm(x, g, b):
    mu = jnp.mean(x, axis=-1, keepdims=True)
    xc = x - mu
    var = jnp.mean(xc * xc, axis=-1, keepdims=True)
    return xc * lax.rsqrt(var + LN_EPS) * g + b


def _sort_key(x):
    bits = lax.bitcast_convert_type(x + 0.0, I32)
    return bits ^ ((bits >> 31) & 0x7FFFFFFF)


def _embed_proj_body(x_ref, g_ref, b_ref, cos_ref, sa_ref, sb_ref,
                     wqa_ref, wckv_ref, wqi_ref, wsm_ref, wwiT_ref, wqf_ref, wkf_ref, wvfT_ref,
                     wga_ref, wgf_ref, kvg_ref, wkk_ref, wkvT_ref, bf_ref,
                     qa_o, ka_o, vaT_o, qi_o, ki2_o, wT_o, qf_o, kf_o, vfT_o, c_o, cT_o, sga_o, sgf_o,
                     carry_ref):
    t = pl.program_id(1)

    @pl.when(t == 0)
    def _():
        carry_ref[...] = jnp.zeros_like(carry_ref)

    h = _layer_norm(x_ref[0], g_ref[...], b_ref[...])
    hb = h.astype(BF16)
    cos, sa, sb = cos_ref[...], sa_ref[...], sb_ref[...]

    def rope(y):
        outs = []
        for c in range(y.shape[1] // LANES):
            yc = y[:, c * LANES:(c + 1) * LANES]
            outs.append(yc * cos + pltpu.roll(yc, LANES - ROT_DIM // 2, 1) * sa
                        + pltpu.roll(yc, ROT_DIM // 2, 1) * sb)
        return outs[0] if len(outs) == 1 else jnp.concatenate(outs, axis=1)

    qa_o[0] = (rope(_dot(hb, wqa_ref[...])) * (HEAD_DIM ** -0.5)).astype(BF16)
    ckv = _dot(hb, wckv_ref[...])
    cn = ckv * lax.rsqrt(jnp.mean(ckv * ckv, axis=-1, keepdims=True) + RMS_EPS) * kvg_ref[...]
    cnb = cn.astype(BF16)
    ka_o[0] = rope(_dot(cnb, wkk_ref[...])).astype(BF16)
    vaT_o[0, 0] = _dot_nt(wkvT_ref[...], cnb).astype(BF16)
    qi_o[0] = rope(_dot(hb, wqi_ref[...])).astype(BF16)
    small = _dot(hb, wsm_ref[...])
    lane = lax.broadcasted_iota(I32, small.shape, 1)
    ki_dup = jnp.where(lane < IDX_DIM, pltpu.roll(small, IDX_DIM, 1), small)
    ki2_o[0] = rope(ki_dup).astype(BF16)
    wT = _dot_nt(wwiT_ref[...], hb)
    wT_o[0] = wT[:N_HEADS] * (N_HEADS ** -0.5 * IDX_DIM ** -0.5)
    qf_o[0] = (_dot(hb, wqf_ref[...]) * (HEAD_DIM ** -0.5)).astype(BF16)
    kf_o[0] = _dot(hb, wkf_ref[...]).astype(BF16)
    vfT_o[0, 0] = _dot_nt(wvfT_ref[...], hb).astype(BF16)
    logf = jnp.where(lane < N_HEADS, jax.nn.log_sigmoid(small + bf_ref[...]), 0.0)
    row = lax.broadcasted_iota(I32, (TM, TM), 0)
    col = lax.broadcasted_iota(I32, (TM, TM), 1)
    tri = (col <= row).astype(F32)
    c = _dot(tri, logf, precision=lax.Precision.HIGHEST) + carry_ref[...]
    c_o[0] = c
    cT_o[0] = c.T[:N_HEADS]
    carry_ref[...] = c[TM - 1:TM, :]
    sga_o[0] = jax.nn.sigmoid(_dot(hb, wga_ref[...])).astype(BF16)
    sgf_o[0] = jax.nn.sigmoid(_dot(hb, wgf_ref[...])).astype(BF16)


def _embed_proj(xin, g, b, cos, sa, sb, wts, kvg, wkk, wkvT, bfr):
    B, Lp, D = xin.shape
    nt = Lp // TM
    (wqa, wckv, wqi, wsm, wwiT, wqf, wkf, wvfT, wga, wgf) = wts

    def full(a):
        return pl.BlockSpec(a.shape, lambda bi, ti: (0,) * a.ndim)

    tok = lambda w: pl.BlockSpec((1, TM, w), lambda bi, ti: (bi, ti, 0))
    tab = pl.BlockSpec((TM, LANES), lambda bi, ti: (ti, 0))
    tposed = pl.BlockSpec((1, 1, HEADS_W, TM), lambda bi, ti: (bi, ti, 0, 0))
    headT = pl.BlockSpec((1, N_HEADS, TM), lambda bi, ti: (bi, 0, ti))
    in_specs = [tok(D), full(g), full(b), tab, tab, tab,
                full(wqa), full(wckv), full(wqi), full(wsm), full(wwiT), full(wqf), full(wkf), full(wvfT),
                full(wga), full(wgf), full(kvg), full(wkk), full(wkvT), full(bfr)]
    sd = jax.ShapeDtypeStruct
    out_shape = [sd((B, Lp, HEADS_W), BF16), sd((B, Lp, HEADS_W), BF16), sd((B, nt, HEADS_W, TM), BF16),
                 sd((B, Lp, HEADS_W), BF16), sd((B, Lp, LANES), BF16), sd((B, N_HEADS, Lp), F32),
                 sd((B, Lp, HEADS_W), BF16), sd((B, Lp, HEADS_W), BF16), sd((B, nt, HEADS_W, TM), BF16),
                 sd((B, Lp, LANES), F32), sd((B, N_HEADS, Lp), F32),
                 sd((B, Lp, D), BF16), sd((B, Lp, D), BF16)]
    out_specs = [tok(HEADS_W), tok(HEADS_W), tposed, tok(HEADS_W), tok(LANES), headT,
                 tok(HEADS_W), tok(HEADS_W), tposed, tok(LANES), headT, tok(D), tok(D)]
    return pl.pallas_call(
        _embed_proj_body,
        grid=(B, nt),
        in_specs=in_specs,
        out_specs=out_specs,
        out_shape=out_shape,
        scratch_shapes=[pltpu.VMEM((1, LANES), F32)],
        compiler_params=pltpu.CompilerParams(dimension_semantics=("arbitrary", "arbitrary"),
                                             vmem_limit_bytes=VMEM_LIMIT),
        name="embed_proj",
    )(xin, g, b, cos, sa, sb, wqa, wckv, wqi, wsm, wwiT, wqf, wkf, wvfT, wga, wgf, kvg, wkk, wkvT, bfr)


def _masked_heads(q_ref, dst_ref):
    lane = lax.broadcasted_iota(I32, (TM, LANES), 1)
    for h in range(N_HEADS):
        blk = q_ref[0, :, (h // 2) * LANES:(h // 2 + 1) * LANES]
        keep = (lane < HEAD_DIM) if h % 2 == 0 else (lane >= HEAD_DIM)
        dst_ref[h] = jnp.where(keep, blk, jnp.zeros_like(blk))


def _count_rows(mask):
    m = mask.astype(I32)
    acc = m[0:SUBLANES]
    for j in range(1, TM // SUBLANES):
        acc = acc + m[j * SUBLANES:(j + 1) * SUBLANES]
    return acc


def _flash_heads(qm_ref, k_ref, vT_ref, n_tiles, bias_fn, o_ref):
    row = lax.broadcasted_iota(I32, (LANES, TM), 0)
    for pair in range(N_HEADS // 2):
        outs = []
        for h in (2 * pair, 2 * pair + 1):
            qh = qm_ref[h]

            def step(kt, carry, h=h, qh=qh, pair=pair):
                m, l, acc = carry
                k0 = pl.multiple_of(kt * TM, TM)
                kt_blk = k_ref[0, pl.ds(k0, TM), pair * LANES:(pair + 1) * LANES]
                s = _dot_nt(kt_blk, qh) + bias_fn(h, kt)
                m_new = jnp.maximum(m, jnp.max(s, axis=0, keepdims=True))
                alpha = jnp.exp(m - m_new)
                p = jnp.exp(s - m_new)
                l = l * alpha + jnp.sum(p, axis=0, keepdims=True)
                vT = vT_ref[0, kt, pair * LANES:(pair + 1) * LANES, :]
                acc = acc * alpha + _dot(vT, p.astype(BF16))
                return m_new, l, acc

            init = (jnp.full((1, TM), NEG, F32), jnp.zeros((1, TM), F32), jnp.zeros((LANES, TM), F32))
            m, l, acc = lax.fori_loop(0, n_tiles, step, init)
            outs.append(acc / l)
        oT = jnp.where(row < HEAD_DIM, outs[0], outs[1])
        o_ref[0, :, pair * LANES:(pair + 1) * LANES] = oT.T.astype(BF16)


def _dsa_body(topk, qi_ref, wT_ref, qa_ref, ki2_ref, ka_ref, vaT_ref, ya_ref, s_ref, qm_ref):
    i = pl.program_id(1)
    n_tiles = i + 1
    q0 = i * TM
    qpos = q0 + lax.broadcasted_iota(I32, (TM, TM), 1)
    krow = lax.broadcasted_iota(I32, (TM, TM), 0)

    _masked_heads(qi_ref, qm_ref)
    w = wT_ref[0]

    def score_step(kt, _):
        k0 = pl.multiple_of(kt * TM, TM)
        kblk = ki2_ref[0, pl.ds(k0, TM), :]
        acc = jnp.zeros((TM, TM), F32)
        for h in range(N_HEADS):
            s = _dot_nt(kblk, qm_ref[h])
            acc = acc + jnp.maximum(s, 0.0) * w[h:h + 1, :]
        acc = jnp.where(k0 + krow <= qpos, acc, NEG)
        s_ref[kt] = _sort_key(acc)
        return 0

    lax.fori_loop(0, n_tiles, score_step, 0)

    def count(pred_fn):
        def body(kt, acc):
            return acc + _count_rows(pred_fn(s_ref[kt], kt))
        acc = lax.fori_loop(0, n_tiles, body, jnp.zeros((SUBLANES, TM), I32))
        return jnp.sum(acc, axis=0, keepdims=True)

    def bit_step(it, u):
        bit = jnp.left_shift(jnp.int32(1), 31 - it)
        cand = (u | bit) ^ INT_MIN
        cnt = count(lambda key, kt: key >= cand)
        return jnp.where(cnt >= topk, u | bit, u)

    u = lax.fori_loop(0, 32, bit_step, jnp.zeros((1, TM), I32))
    thr = u ^ INT_MIN
    need = topk - count(lambda key, kt: key > thr)

    def pos_step(it, j):
        bit = jnp.left_shift(jnp.int32(1), 14 - it)
        cand = j | bit
        cnt = count(lambda key, kt: (key == thr) & (kt * TM + krow < cand))
        return jnp.where(cnt < need, cand, j)

    jmax = lax.fori_loop(0, 15, pos_step, jnp.zeros((1, TM), I32))
    jmax = jnp.where(thr > NEG_KEY, jmax, -1)

    def bias_step(kt, _):
        key = s_ref[kt]
        sel = (key > thr) | ((key == thr) & (kt * TM + krow <= jmax))
        s_ref[kt] = lax.bitcast_convert_type(jnp.where(sel, 0.0, NEG).astype(F32), I32)
        return 0

    lax.fori_loop(0, n_tiles, bias_step, 0)
    _masked_heads(qa_ref, qm_ref)
    _flash_heads(qm_ref, ka_ref, vaT_ref, n_tiles,
                 lambda h, kt: lax.bitcast_convert_type(s_ref[kt], F32), ya_ref)


def _dsa_attn(qi, wT, qa, ki2, ka, vaT, topk):
    B, Lp, _ = qi.shape
    nt = Lp // TM
    tok = lambda w: pl.BlockSpec((1, TM, w), lambda bi, ti: (bi, ti, 0))
    whole = lambda a: pl.BlockSpec((1,) + a.shape[1:], lambda bi, ti: (bi,) + (0,) * (a.ndim - 1),
                                   pipeline_mode=pl.Buffered(1))
    return pl.pallas_call(
        functools.partial(_dsa_body, topk),
        grid=(B, nt),
        in_specs=[tok(HEADS_W), pl.BlockSpec((1, N_HEADS, TM), lambda bi, ti: (bi, 0, ti)), tok(HEADS_W),
                  whole(ki2), whole(ka), whole(vaT)],
        out_specs=tok(HEADS_W),
        out_shape=jax.ShapeDtypeStruct((B, Lp, HEADS_W), BF16),
        scratch_shapes=[pltpu.VMEM((nt, TM, TM), I32), pltpu.VMEM((N_HEADS, TM, LANES), BF16)],
        compiler_params=pltpu.CompilerParams(dimension_semantics=("arbitrary", "arbitrary"),
                                             vmem_limit_bytes=VMEM_LIMIT),
        name="dsa_attn",
    )(qi, wT, qa, ki2, ka, vaT)


def _fox_body(qf_ref, cT_ref, kf_ref, vfT_ref, c_ref, yf_ref, qm_ref):
    i = pl.program_id(1)
    qpos = i * TM + lax.broadcasted_iota(I32, (TM, TM), 1)
    krow = lax.broadcasted_iota(I32, (TM, TM), 0)
    _masked_heads(qf_ref, qm_ref)
    cq = cT_ref[0]

    def bias(h, kt):
        k0 = pl.multiple_of(kt * TM, TM)
        ck = c_ref[0, pl.ds(k0, TM), :][:, h:h + 1]
        return jnp.where(k0 + krow <= qpos, cq[h:h + 1, :] - ck, NEG)

    _flash_heads(qm_ref, kf_ref, vfT_ref, i + 1, bias, yf_ref)


def _fox_attn(qf, cT, kf, vfT, c):
    B, Lp, _ = qf.shape
    nt = Lp // TM
    tok = lambda w: pl.BlockSpec((1, TM, w), lambda bi, ti: (bi, ti, 0))
    whole = lambda a: pl.BlockSpec((1,) + a.shape[1:], lambda bi, ti: (bi,) + (0,) * (a.ndim - 1),
                                   pipeline_mode=pl.Buffered(1))
    return pl.pallas_call(
        _fox_body,
        grid=(B, nt),
        in_specs=[tok(HEADS_W), pl.BlockSpec((1, N_HEADS, TM), lambda bi, ti: (bi, 0, ti)),
                  whole(kf), whole(vfT), whole(c)],
        out_specs=tok(HEADS_W),
        out_shape=jax.ShapeDtypeStruct((B, Lp, HEADS_W), BF16),
        scratch_shapes=[pltpu.VMEM((N_HEADS, TM, LANES), BF16)],
        compiler_params=pltpu.CompilerParams(dimension_semantics=("arbitrary", "arbitrary"),
                                             vmem_limit_bytes=VMEM_LIMIT),
        name="fox_attn",
    )(qf, cT, kf, vfT, c)


def _post_mix_body(n_real, nt, x_ref, ya_ref, yf_ref, sga_ref, sgf_ref, wbd_ref, wbf_ref, wo_ref,
                   eg_ref, eb_ref, g1_ref, b1_ref, wr_ref, br_ref,
                   h1s_ref, route_ref, gate_ref, cnt_ref, carry_ref):
    pid = pl.program_id(0)

    @pl.when(pid == 0)
    def _():
        carry_ref[...] = jnp.zeros_like(carry_ref)

    h = _layer_norm(x_ref[...], eg_ref[...], eb_ref[...])
    merged = (sga_ref[...].astype(F32) * _dot(ya_ref[...], wbd_ref[...])
              + sgf_ref[...].astype(F32) * _dot(yf_ref[...], wbf_ref[...]))
    mix = _dot(merged.astype(BF16), wo_ref[...])
    h1 = _layer_norm(DN_ALPHA * h + mix, g1_ref[...], b1_ref[...])
    for s in range(SUBLANES):
        h1s_ref[pl.ds(s, TM, stride=SUBLANES), :] = h1[:, s * LANES:(s + 1) * LANES]

    lg = _dot(h1, wr_ref[...], precision=lax.Precision.HIGHEST) + br_ref[...]
    lane = lax.broadcasted_iota(I32, (TM, LANES), 1)
    gmask = lane < N_GROUPS
    gl = jnp.where(gmask, lg, -jnp.inf)
    gmax = jnp.max(gl, axis=-1, keepdims=True)
    g_sel = jnp.min(jnp.where(gl == gmax, lane, LANES), axis=-1, keepdims=True)
    p_group = 1.0 / jnp.sum(jnp.where(gmask, jnp.exp(lg - gmax), 0.0), axis=-1, keepdims=True)
    e_lo = EXPERT_LANE0 + g_sel * EXPERTS_PER_GROUP
    emask = (lane >= e_lo) & (lane < e_lo + EXPERTS_PER_GROUP)
    el = jnp.where(emask, lg, -jnp.inf)
    top1 = jnp.max(el, axis=-1, keepdims=True)
    i1 = jnp.min(jnp.where(el == top1, lane, LANES), axis=-1, keepdims=True)
    el2 = jnp.where(lane == i1, -jnp.inf, el)
    top2 = jnp.max(el2, axis=-1, keepdims=True)
    i2 = jnp.min(jnp.where(el2 == top2, lane, LANES), axis=-1, keepdims=True)
    e21 = jnp.exp(top2 - top1)
    gate1 = p_group / (1.0 + e21)
    gate2 = p_group * e21 / (1.0 + e21)

    tpos = (pid % nt) * TM + lax.broadcasted_iota(I32, (TM, 1), 0)
    real = tpos < n_real
    onehot = jnp.where(real & ((lane == i1) | (lane == i2)), 1.0, 0.0)
    row = lax.broadcasted_iota(I32, (TM, TM), 0)
    col = lax.broadcasted_iota(I32, (TM, TM), 1)
    before = _dot((col < row).astype(BF16), onehot.astype(BF16)) + carry_ref[...]
    r1 = jnp.sum(jnp.where(lane == i1, before, 0.0), axis=-1, keepdims=True).astype(I32)
    r2 = jnp.sum(jnp.where(lane == i2, before, 0.0), axis=-1, keepdims=True).astype(I32)
    carry = carry_ref[...] + jnp.sum(onehot, axis=0, keepdims=True)
    carry_ref[...] = carry
    cnt_ref[...] = carry
    route = jnp.where(lane == 0, i1 - EXPERT_LANE0,
                      jnp.where(lane == 1, i2 - EXPERT_LANE0,
                                jnp.where(lane == 2, r1, jnp.where(lane == 3, r2, 0))))
    route_ref[...] = route
    gate_ref[...] = jnp.where(lane == 0, gate1, jnp.where(lane == 1, gate2, 0.0))


def _post_mix(xin2, ya2, yf2, sga2, sgf2, wbd, wbf, wo, eg, eb, g1, b1, wr, br, n_real, nt):
    N, D = xin2.shape
    full = lambda a: pl.BlockSpec(a.shape, lambda i: (0,) * a.ndim)
    tok = lambda w: pl.BlockSpec((TM, w), lambda i: (i, 0))
    sd = jax.ShapeDtypeStruct
    return pl.pallas_call(
        functools.partial(_post_mix_body, n_real, nt),
        grid=(N // TM,),
        in_specs=[tok(D), tok(HEADS_W), tok(HEADS_W), tok(D), tok(D), full(wbd), full(wbf), full(wo),
                  full(eg), full(eb), full(g1), full(b1), full(wr), full(br)],
        out_specs=[pl.BlockSpec((TM * SUBLANES, LANES), lambda i: (i, 0)), tok(LANES), tok(LANES),
                   pl.BlockSpec((1, LANES), lambda i: (0, 0))],
        out_shape=[sd((N * SUBLANES, LANES), F32), sd((N, LANES), I32), sd((N, LANES), F32),
                   sd((1, LANES), F32)],
        scratch_shapes=[pltpu.VMEM((1, LANES), F32)],
        compiler_params=pltpu.CompilerParams(dimension_semantics=("arbitrary",),
                                             vmem_limit_bytes=VMEM_LIMIT),
        name="post_mix",
    )(xin2, ya2, yf2, sga2, sgf2, wbd, wbf, wo, eg, eb, g1, b1, wr, br)


DISPATCH_CHUNK = 512


def _row_copy(src_ref, src_row, dst_ref, dst_row, sem):
    return pltpu.make_async_copy(src_ref.at[pl.ds(pl.multiple_of(src_row * SUBLANES, SUBLANES), SUBLANES)],
                                 dst_ref.at[pl.ds(pl.multiple_of(dst_row * SUBLANES, SUBLANES), SUBLANES)],
                                 sem)


def _dispatch_body(n_real, lp, d0_ref, d1_ref, h1s_ref, zeros_ref, buf_ref, sem):
    del zeros_ref
    base = pl.program_id(0) * DISPATCH_CHUNK

    def copies(n):
        return (_row_copy(h1s_ref, n, buf_ref, d0_ref[n], sem), _row_copy(h1s_ref, n, buf_ref, d1_ref[n], sem))

    def start(j, _):
        n = base + j

        @pl.when(n % lp < n_real)
        def _():
            for cp in copies(n):
                cp.start()
        return 0

    def wait(j, _):
        n = base + j

        @pl.when(n % lp < n_real)
        def _():
            for cp in copies(n):
                cp.wait()
        return 0

    lax.fori_loop(0, DISPATCH_CHUNK, start, 0)
    lax.fori_loop(0, DISPATCH_CHUNK, wait, 0)


def _dispatch(d0, d1, h1s, zeros_buf, n_real, lp):
    n = d0.shape[0]
    any_spec = pl.BlockSpec(memory_space=pl.ANY)
    return pl.pallas_call(
        functools.partial(_dispatch_body, n_real, lp),
        grid_spec=pltpu.PrefetchScalarGridSpec(
            num_scalar_prefetch=2, grid=(n // DISPATCH_CHUNK,),
            in_specs=[any_spec, any_spec], out_specs=any_spec,
            scratch_shapes=[pltpu.SemaphoreType.DMA(())]),
        out_shape=jax.ShapeDtypeStruct(zeros_buf.shape, F32),
        input_output_aliases={3: 0},
        compiler_params=pltpu.CompilerParams(dimension_semantics=("arbitrary",)),
        name="moe_dispatch",
    )(d0, d1, h1s, zeros_buf)


def _expert_body(blk_e_ref, x_ref, wg_ref, wu_ref, wd_ref, y_ref):
    del blk_e_ref
    x = jnp.concatenate([x_ref[pl.ds(s, MOE_ROWS, stride=SUBLANES), :] for s in range(SUBLANES)], axis=1)
    xb = x.astype(BF16)
    g = _dot(xb, wg_ref[0])
    u = _dot(xb, wu_ref[0])
    y = _dot((g * jax.nn.sigmoid(g) * u).astype(BF16), wd_ref[0])
    for s in range(SUBLANES):
        y_ref[pl.ds(s, MOE_ROWS, stride=SUBLANES), :] = y[:, s * LANES:(s + 1) * LANES]


def _expert_ffn(blk_e, buf, wg, wu, wd):
    n_blocks = blk_e.shape[0]
    D = wg.shape[1]
    slab = pl.BlockSpec((MOE_ROWS * SUBLANES, LANES), lambda i, be: (i, 0))
    return pl.pallas_call(
        _expert_body,
        grid_spec=pltpu.PrefetchScalarGridSpec(
            num_scalar_prefetch=1, grid=(n_blocks,),
            in_specs=[slab,
                      pl.BlockSpec((1, D, D_EXPERT), lambda i, be: (be[i], 0, 0)),
                      pl.BlockSpec((1, D, D_EXPERT), lambda i, be: (be[i], 0, 0)),
                      pl.BlockSpec((1, D_EXPERT, D), lambda i, be: (be[i], 0, 0))],
            out_specs=slab),
        out_shape=jax.ShapeDtypeStruct(buf.shape, F32),
        compiler_params=pltpu.CompilerParams(dimension_semantics=("arbitrary",),
                                             vmem_limit_bytes=VMEM_LIMIT),
        name="expert_ffn",
    )(blk_e, buf, wg, wu, wd)


def _combine_body(d0_ref, d1_ref, h1s_ref, gate_ref, g2_ref, b2_ref, y_ref, out_ref, y0_ref, y1_ref, sem):
    base = pl.program_id(0) * TM

    def copies(j):
        n = base + j
        return (_row_copy(y_ref, d0_ref[n], y0_ref, j, sem), _row_copy(y_ref, d1_ref[n], y1_ref, j, sem))

    def start(j, _):
        for cp in copies(j):
            cp.start()
        return 0

    def wait(j, _):
        for cp in copies(j):
            cp.wait()
        return 0

    lax.fori_loop(0, TM, start, 0)
    lax.fori_loop(0, TM, wait, 0)
    gt = gate_ref[...]
    w0, w1 = gt[:, 0:1], gt[:, 1:2]
    zs = []
    for s in range(SUBLANES):
        sl = pl.ds(s, TM, stride=SUBLANES)
        zs.append(DN_ALPHA * h1s_ref[sl, :] + (w0 * y0_ref[sl, :] + w1 * y1_ref[sl, :]))
    d = SUBLANES * LANES
    mu = sum(jnp.sum(z, axis=-1, keepdims=True) for z in zs) / d
    var = sum(jnp.sum((z - mu) * (z - mu), axis=-1, keepdims=True) for z in zs) / d
    inv = lax.rsqrt(var + LN_EPS)
    for s in range(SUBLANES):
        cs = slice(s * LANES, (s + 1) * LANES)
        out_ref[:, cs] = (zs[s] - mu) * inv * g2_ref[:, cs] + b2_ref[:, cs]


def _combine(d0, d1, h1s, gate, g2, b2, y):
    n = d0.shape[0]
    D = g2.shape[1]
    return pl.pallas_call(
        _combine_body,
        grid_spec=pltpu.PrefetchScalarGridSpec(
            num_scalar_prefetch=2, grid=(n // TM,),
            in_specs=[pl.BlockSpec((TM * SUBLANES, LANES), lambda i, a, b: (i, 0)),
                      pl.BlockSpec((TM, LANES), lambda i, a, b: (i, 0)),
                      pl.BlockSpec((1, D), lambda i, a, b: (0, 0)),
                      pl.BlockSpec((1, D), lambda i, a, b: (0, 0)),
                      pl.BlockSpec(memory_space=pl.ANY)],
            out_specs=pl.BlockSpec((TM, D), lambda i, a, b: (i, 0)),
            scratch_shapes=[pltpu.VMEM((TM * SUBLANES, LANES), F32), pltpu.VMEM((TM * SUBLANES, LANES), F32),
                            pltpu.SemaphoreType.DMA(())]),
        out_shape=jax.ShapeDtypeStruct((n, D), F32),
        compiler_params=pltpu.CompilerParams(dimension_semantics=("arbitrary",),
                                             vmem_limit_bytes=VMEM_LIMIT),
        name="moe_combine",
    )(d0, d1, h1s, gate, g2, b2, y)


def _rope_tables(lp):
    half = ROT_DIM // 2
    inv = ROPE_THETA ** (-jnp.arange(half, dtype=F32) / half)
    ang = jnp.arange(lp, dtype=F32)[:, None] * inv[None, :]
    cos, sin = jnp.cos(ang), jnp.sin(ang)
    ones = jnp.ones((lp, HEAD_DIM - ROT_DIM), F32)
    zeros = jnp.zeros((lp, HEAD_DIM - ROT_DIM), F32)
    z8 = jnp.zeros((lp, half), F32)
    c64 = jnp.concatenate([cos, cos, ones], axis=1)
    sa64 = jnp.concatenate([-sin, z8, zeros], axis=1)
    sb64 = jnp.concatenate([z8, sin, zeros], axis=1)
    rep = lambda a: jnp.concatenate([a, a], axis=1)
    return rep(c64), rep(sa64), rep(sb64)


def kernel(x, meta_tokens, emb_ln_g, emb_ln_b, w_in, b_forget, kv_norm_g, w_kv_up, w_branch_dsa, w_branch_fox,
           w_out, ln1_g, ln1_b, w_router_group, b_router_group, w_router_expert, b_router_expert, w_gate, w_up,
           w_down, ln2_g, ln2_b):
    B, S, D = x.shape
    assert w_in.shape[0] == DEPTH == 1
    T = N_META + S
    Lp = -(-T // TM) * TM
    nt = Lp // TM
    topk = min(TOPK_MAX, T // 4)
    row = lambda a: a.reshape(1, -1)

    meta = jnp.broadcast_to(meta_tokens.astype(x.dtype)[None], (B, N_META, D))
    xin = jnp.concatenate([meta, x, jnp.zeros((B, Lp - T, D), x.dtype)], axis=1)

    w = w_in[0]
    offs = np.cumsum([0, HEADS_W, KV_RANK, HEADS_W, IDX_DIM, N_HEADS, HEADS_W, HEADS_W, HEADS_W, N_HEADS, D, D])
    cols = [w[:, offs[i]:offs[i + 1]] for i in range(11)]
    w_qa, w_ckv, w_qi, w_ki, w_wi, w_qf, w_kf, w_vf, w_fl, w_ga, w_gf = cols
    w_small = jnp.concatenate([w_fl, jnp.zeros((D, IDX_DIM - N_HEADS), F32), w_ki], axis=1)
    w_wiT = jnp.concatenate([w_wi.T, jnp.zeros((16 - N_HEADS, D), F32)], axis=0)
    bf_row = jnp.concatenate([b_forget[0], jnp.zeros((LANES - N_HEADS,), F32)]).reshape(1, LANES)
    wkv = w_kv_up[0]
    wts = tuple(a.astype(BF16) for a in (w_qa, w_ckv, w_qi, w_small, w_wiT, w_qf, w_kf, w_vf.T, w_ga, w_gf))
    cos, sa, sb = _rope_tables(Lp)
    (qa, ka, vaT, qi, ki2, wT, qf, kf, vfT, c, cT, sga, sgf) = _embed_proj(
        xin, row(emb_ln_g), row(emb_ln_b), cos, sa, sb, wts, row(kv_norm_g[0]),
        wkv[:, :HEADS_W].astype(BF16), wkv[:, HEADS_W:].T.astype(BF16), bf_row)

    ya = _dsa_attn(qi, wT, qa, ki2, ka, vaT, topk)
    yf = _fox_attn(qf, cT, kf, vfT, c)

    n_tok = B * Lp
    w_r = jnp.concatenate([w_router_group[0], w_router_expert[0],
                           jnp.zeros((D, LANES - N_GROUPS - N_EXPERTS), F32)], axis=1)
    b_r = jnp.concatenate([b_router_group[0], b_router_expert[0],
                           jnp.zeros((LANES - N_GROUPS - N_EXPERTS,), F32)]).reshape(1, LANES)
    flat = lambda a: a.reshape(n_tok, a.shape[-1])
    h1s, route, gate, cnt = _post_mix(
        flat(xin), flat(ya), flat(yf), flat(sga), flat(sgf), w_branch_dsa[0].astype(BF16),
        w_branch_fox[0].astype(BF16), w_out[0].astype(BF16), row(emb_ln_g), row(emb_ln_b), row(ln1_g[0]),
        row(ln1_b[0]), w_r, b_r, T, nt)

    counts = cnt[0, EXPERT_LANE0:EXPERT_LANE0 + N_EXPERTS].astype(I32)
    pcounts = (counts + MOE_ROWS - 1) // MOE_ROWS * MOE_ROWS
    pend = jnp.cumsum(pcounts)
    pstart = pend - pcounts
    n_blocks = (B * T * 2 + N_EXPERTS * (MOE_ROWS - 1) + MOE_ROWS - 1) // MOE_ROWS
    real = (jnp.arange(n_tok) % Lp) < T
    e0, e1, r0, r1 = route[:, 0], route[:, 1], route[:, 2], route[:, 3]
    d0 = jnp.where(real, pstart[jnp.clip(e0, 0, N_EXPERTS - 1)] + r0, 0).astype(I32)
    d1 = jnp.where(real, pstart[jnp.clip(e1, 0, N_EXPERTS - 1)] + r1, 0).astype(I32)
    blk_e = jnp.minimum(jnp.searchsorted(pend, jnp.arange(n_blocks) * MOE_ROWS, side='right'),
                        N_EXPERTS - 1).astype(I32)

    buf = _dispatch(d0, d1, h1s, jnp.zeros((n_blocks * MOE_ROWS * SUBLANES, LANES), F32), T, Lp)
    y = _expert_ffn(blk_e, buf, w_gate[0].astype(BF16), w_up[0].astype(BF16), w_down[0].astype(BF16))
    out = _combine(d0, d1, h1s, gate, row(ln2_g[0]), row(ln2_b[0]), y)
    return out.reshape(B, Lp, D)[:, N_META:T]
```
